```python
import math
import jax, jax.numpy as jnp
from jax import lax
import numpy as np

D_MODEL = 1024
BATCH = 4
SEQ = 4096
DEPTH = 1

PLE_DIM = 256
HEAD_DIM = 64
ATTN_HEADS = 8
KV_HEADS = 2
ATTN_WIDTH = ATTN_HEADS * HEAD_DIM
IDX_HEADS = 4
IDX_DIM = 64
TOPK_MAX = 256
Q_BLOCK = 128
POOL_WINDOWS = (2, 4, 8, 16)
POOL_GROUPS = len(POOL_WINDOWS)
POOL_WIDTH = D_MODEL - ATTN_WIDTH
POOL_GROUP_DIM = POOL_WIDTH // POOL_GROUPS
MIX_WIDTH = ATTN_WIDTH + POOL_WIDTH
D_FF = 4 * D_MODEL
ROPE_THETA = 500000.0
ROT_DIM = HEAD_DIM // 4
EPS = 1e-6
ATTN_SCALE = HEAD_DIM ** -0.5
IDX_SCALE = IDX_DIM ** -0.5
IN_WIDTHS = (ATTN_HEADS * HEAD_DIM,
             KV_HEADS * HEAD_DIM,
             KV_HEADS * HEAD_DIM,
             IDX_HEADS * IDX_DIM,
             IDX_DIM,
             IDX_HEADS,
             POOL_WIDTH)
IN_WIDTH = sum(IN_WIDTHS)
IN_SPLITS = tuple(int(v) for v in np.cumsum(IN_WIDTHS)[:-1])

kernel_name = "hybrid_dsa_pool_parallel_block"


def rmsnorm(x, g):
    xf = x.astype(jnp.float32)
    y = xf * lax.rsqrt(jnp.mean(xf * xf, axis=-1, keepdims=True) + EPS)
    return (y * g.astype(jnp.float32)).astype(x.dtype)


def rope_tables(positions):
    inv_freq = ROPE_THETA ** (-jnp.arange(0, ROT_DIM, 2, dtype=jnp.float32) / ROT_DIM)
    ang = positions.astype(jnp.float32)[..., None] * inv_freq
    return jnp.cos(ang)[:, :, None, :], jnp.sin(ang)[:, :, None, :]


def rope_partial(x, cos, sin):
    half = ROT_DIM // 2
    x1, x2, rest = x[..., :half], x[..., half:ROT_DIM], x[..., ROT_DIM:]
    c, s = cos.astype(x.dtype), sin.astype(x.dtype)
    return jnp.concatenate([x1 * c - x2 * s, x2 * c + x1 * s, rest], axis=-1)


def dsa_attention(q, k, v, q_idx, k_idx, w_idx):
    B, S, H, Dh = q.shape
    G = k.shape[2]
    R = H // G
    n_keep = min(TOPK_MAX, S // 4)
    nb = S // Q_BLOCK
    key_pos = jnp.arange(S)
    q_pos = key_pos.reshape(nb, Q_BLOCK)
    gather = jax.vmap(lambda table, idx: table[idx])

    def to_blocks(a):
        return a.reshape((B, nb, Q_BLOCK) + a.shape[2:]).swapaxes(0, 1)

    def block(args):
        qb, qib, wb, tq = args
        logits = jnp.einsum('bqhd,bsd->bqhs', qib, k_idx).astype(jnp.float32) * IDX_SCALE
        score = jnp.einsum('bqh,bqhs->bqs', wb.astype(jnp.float32), jax.nn.relu(logits))
        causal = key_pos[None, :] <= tq[:, None]
        score = jnp.where(causal[None], score, -jnp.inf)
        _, sel = lax.top_k(score, n_keep)
        valid = sel <= tq[None, :, None]
        kg = gather(k, sel)
        vg = gather(v, sel)
        qg = qb.reshape(B, Q_BLOCK, G, R, Dh)
        s = jnp.einsum('bqgrd,bqkgd->bqgrk', qg, kg).astype(jnp.float32) * ATTN_SCALE
        s = jnp.where(valid[:, :, None, None, :], s, -jnp.inf)
        pr = jax.nn.softmax(s, axis=-1).astype(vg.dtype)
        o = jnp.einsum('bqgrk,bqkgd->bqgrd', pr, vg)
        return o.reshape(B, Q_BLOCK, H * Dh)

    out = lax.map(block, (to_blocks(q), to_blocks(q_idx), to_blocks(w_idx), q_pos))
    return out.swapaxes(0, 1).reshape(B, S, H * Dh)


def pool_mixer(u, w_grp, scale):
    B, S, C = u.shape
    ug = u.reshape(B, S, POOL_GROUPS, POOL_GROUP_DIM)
    csum = jnp.cumsum(ug.astype(jnp.float32), axis=1)
    t = jnp.arange(S)
    means = []
    for g, w in enumerate(POOL_WINDOWS):
        cg = csum[:, :, g]
        lag = jnp.pad(cg, ((0, 0), (w, 0), (0, 0)))[:, :S]
        cnt = jnp.minimum(t + 1, w).astype(jnp.float32)[None, :, None]
        means.append((cg - lag) / cnt)
    mean = jnp.stack(means, axis=2)
    r = (mean - ug.astype(jnp.float32)).astype(u.dtype)
    y = jnp.einsum('bsgc,gcd->bsgd', r, w_grp)
    return y.reshape(B, S, C) * scale


def setup_inputs(seed: int = 0) -> dict:
    key = jax.random.key(seed)
    ks = jax.random.split(key, 20)
    f32 = jnp.float32

    def nrm(k, shape, fan_in):
        return jax.random.normal(k, shape, f32) * (fan_in ** -0.5)

    def gain(k, n):
        return 1.0 + 0.05 * jax.random.normal(k, (DEPTH, n), f32)

    x = jax.random.normal(ks[0], (BATCH, SEQ, D_MODEL), f32)
    p = jax.random.normal(ks[1], (DEPTH, BATCH, SEQ, PLE_DIM), f32)
    offset = jax.random.randint(ks[2], (BATCH, 1), 0, 4096, dtype=jnp.int32)
    positions = (offset + jnp.arange(SEQ, dtype=jnp.int32)[None, :]).astype(jnp.int32)
    return {
        "x": x,
        "p": p,
        "positions": positions,
        "g_mix_pre": gain(ks[3], D_MODEL),
        "w_in": nrm(ks[4], (DEPTH, D_MODEL, IN_WIDTH), D_MODEL),
        "w_pool": nrm(ks[5], (DEPTH, POOL_GROUPS, POOL_GROUP_DIM, POOL_GROUP_DIM), POOL_GROUP_DIM),
        "pool_scale": 1.0 + 0.1 * jax.random.normal(ks[6], (DEPTH, POOL_WIDTH), f32),
        "w_o": nrm(ks[7], (DEPTH, MIX_WIDTH, D_MODEL), MIX_WIDTH),
        "g_mix_post": gain(ks[8], D_MODEL),
        "g_mlp_pre": gain(ks[9], D_MODEL),
        "w_up": nrm(ks[10], (DEPTH, D_MODEL, D_FF), D_MODEL),
        "w_down": nrm(ks[11], (DEPTH, D_FF, D_MODEL), D_FF),
        "g_mlp_post": gain(ks[12], D_MODEL),
        "w_ple_gate": nrm(ks[13], (DEPTH, D_MODEL, D_MODEL), D_MODEL),
        "w_ple_proj": nrm(ks[14], (DEPTH, PLE_DIM, D_MODEL), PLE_DIM),
        "g_ple_post": gain(ks[15], D_MODEL),
    }


def reference(x, p, positions, g_mix_pre, w_in, w_pool, pool_scale, w_o, g_mix_post,
              g_mlp_pre, w_up, w_down, g_mlp_post, w_ple_gate, w_ple_proj, g_ple_post):
    B, S, _ = x.shape
    cos, sin = rope_tables(positions)
    h = x
    for i in range(DEPTH):
        hn = rmsnorm(h, g_mix_pre[i])
        z = hn @ w_in[i]
        zq, zk, zv, zqi, zki, zw, zp = jnp.split(z, IN_SPLITS, axis=-1)
        q = rope_partial(zq.reshape(B, S, ATTN_HEADS, HEAD_DIM), cos, sin)
        k = rope_partial(zk.reshape(B, S, KV_HEADS, HEAD_DIM), cos, sin)
        v = zv.reshape(B, S, KV_HEADS, HEAD_DIM)
        qi = rope_partial(zqi.reshape(B, S, IDX_HEADS, IDX_DIM), cos, sin)
        ki = rope_partial(zki.reshape(B, S, 1, IDX_DIM), cos, sin)[:, :, 0]
        wi = zw * (IDX_HEADS ** -0.5)
        attn_out = dsa_attention(q, k, v, qi, ki, wi)
        pool_out = pool_mixer(zp, w_pool[i], pool_scale[i])
        mix = jnp.concatenate([attn_out, pool_out], axis=-1) @ w_o[i]
        h = h + rmsnorm(mix, g_mix_post[i])
        hn = rmsnorm(h, g_mlp_pre[i])
        m = jnp.square(jax.nn.relu(hn @ w_up[i])) @ w_down[i]
        h = h + rmsnorm(m, g_mlp_post[i])
        gate = jax.nn.sigmoid(h @ w_ple_gate[i])
        e = p[i] @ w_ple_proj[i]
        h = h + rmsnorm(gate * e, g_ple_post[i])
    return h
```

```python
import functools

import jax
import jax.numpy as jnp
from jax import lax
from jax.experimental import pallas as pl
from jax.experimental.pallas import tpu as pltpu

F32 = jnp.float32
BF16 = jnp.bfloat16
I32 = jnp.int32

D_MODEL = 1024
PLE_DIM = 256
HEAD_DIM = 64
ATTN_HEADS = 8
KV_HEADS = 2
HEADS_PER_KV = ATTN_HEADS // KV_HEADS
ATTN_WIDTH = ATTN_HEADS * HEAD_DIM
IDX_HEADS = 4
IDX_DIM = 64
TOPK_MAX = 256
POOL_WINDOWS = (2, 4, 8, 16)
POOL_GROUPS = len(POOL_WINDOWS)
POOL_WIDTH = D_MODEL - ATTN_WIDTH
POOL_GROUP_DIM = POOL_WIDTH // POOL_GROUPS
POOL_HALO = max(POOL_WINDOWS)
D_FF = 4 * D_MODEL
ROPE_THETA = 500000.0
ROT_DIM = HEAD_DIM // 4
ROT_HALF = ROT_DIM // 2
EPS = 1e-6
ATTN_SCALE = HEAD_DIM ** -0.5
IDX_SCALE = IDX_DIM ** -0.5
IDX_W_SCALE = IDX_HEADS ** -0.5

LANES = 128
VMEM_LIMIT_BYTES = 56 * 1024 * 1024

COL_Q = 0
COL_K = COL_Q + ATTN_WIDTH
COL_V = COL_K + KV_HEADS * HEAD_DIM
COL_QI = COL_V + KV_HEADS * HEAD_DIM
COL_KI = COL_QI + IDX_HEADS * IDX_DIM
COL_W = COL_KI + IDX_DIM
W_LANE = COL_W - COL_KI
COL_POOL = COL_KI + LANES
PROJ_WIDTH = COL_POOL + POOL_WIDTH
SRC_POOL = COL_W + IDX_HEADS

TM = 512
KC = 512
TQ = 128
TT = 256
FF_CHUNK = 1024

INT_MIN = -2 ** 31
NEG_BIG = -1e30


def _rms(x, g):
    ms = jnp.mean(x * x, axis=-1, keepdims=True)
    return x * lax.rsqrt(ms + EPS) * g


def _proj_kernel(x_ref, pos_ref, g_ref, w_ref, freq_ref, sa_ref, sb_ref, wp_ref, ps_ref,
                 q_ref, kt_ref, v_ref, qi_ref, kit_ref, wi_ref, pool_ref, zp_scr):
    i = pl.program_id(1)
    hn = _rms(x_ref[...], g_ref[...]).astype(BF16)
    z = jnp.dot(hn, w_ref[...], preferred_element_type=F32)

    ang = pos_ref[...].astype(F32) * freq_ref[...]
    cos = jnp.cos(ang)
    sin = jnp.sin(ang)
    sa = sin * sa_ref[...]
    sb = sin * sb_ref[...]

    def rope(t):
        up = pltpu.roll(t, LANES - ROT_HALF, 1)
        dn = pltpu.roll(t, ROT_HALF, 1)
        return t * cos + up * sa + dn * sb

    for m in range(ATTN_WIDTH // LANES):
        t = rope(z[:, COL_Q + m * LANES:COL_Q + (m + 1) * LANES]) * ATTN_SCALE
        q_ref[2 * m] = t[:, :HEAD_DIM].astype(BF16)
        q_ref[2 * m + 1] = t[:, HEAD_DIM:].astype(BF16)
    kt_ref[...] = rope(z[:, COL_K:COL_V]).T.astype(BF16)
    v_ref[...] = z[:, COL_V:COL_QI].astype(BF16)
    for m in range(IDX_HEADS * IDX_DIM // LANES):
        t = rope(z[:, COL_QI + m * LANES:COL_QI + (m + 1) * LANES])
        qi_ref[2 * m] = t[:, :IDX_DIM].astype(BF16)
        qi_ref[2 * m + 1] = t[:, IDX_DIM:].astype(BF16)
    zki = z[:, COL_KI:COL_POOL]
    kit_ref[...] = rope(zki).T[:IDX_DIM].astype(BF16)
    wi_ref[...] = zki * (IDX_W_SCALE * IDX_SCALE)

    zp = z[:, COL_POOL:]

    @pl.when(i == 0)
    def _():
        zp_scr[0:POOL_HALO] = jnp.zeros((POOL_HALO, POOL_WIDTH), F32)

    @pl.when(i > 0)
    def _():
        zp_scr[0:POOL_HALO] = zp_scr[TM:TM + POOL_HALO]

    zp_scr[POOL_HALO:POOL_HALO + TM] = zp
    t_idx = i * TM + lax.broadcasted_iota(I32, (TM, 1), 0)
    for g, win in enumerate(POOL_WINDOWS):
        c0 = g * POOL_GROUP_DIM
        acc = zp[:, c0:c0 + POOL_GROUP_DIM]
        for d in range(1, win):
            acc = acc + zp_scr[POOL_HALO - d:POOL_HALO - d + TM, c0:c0 + POOL_GROUP_DIM]
        cnt = jnp.minimum(t_idx + 1, win).astype(F32)
        r = acc / cnt - zp[:, c0:c0 + POOL_GROUP_DIM]
        y = jnp.dot(r.astype(BF16), wp_ref[g], preferred_element_type=F32)
        pool_ref[:, c0:c0 + POOL_GROUP_DIM] = (y * ps_ref[:, c0:c0 + POOL_GROUP_DIM]).astype(BF16)


def _proj_call(x, pos, g, w, freq, sa, sb, wp, ps):
    B, S, _ = x.shape
    nt = S // TM
    const2 = lambda b, i: (0, 0)
    return pl.pallas_call(
        _proj_kernel,
        grid=(B, nt),
        in_specs=[
            pl.BlockSpec((None, TM, D_MODEL), lambda b, i: (b, i, 0)),
            pl.BlockSpec((None, TM, 1), lambda b, i: (b, i, 0)),
            pl.BlockSpec((1, D_MODEL), const2),
            pl.BlockSpec((D_MODEL, PROJ_WIDTH), const2),
            pl.BlockSpec((1, LANES), const2),
            pl.BlockSpec((1, LANES), const2),
            pl.BlockSpec((1, LANES), const2),
            pl.BlockSpec((POOL_GROUPS, POOL_GROUP_DIM, POOL_GROUP_DIM), lambda b, i: (0, 0, 0)),
            pl.BlockSpec((1, POOL_WIDTH), const2),
        ],
        out_specs=[
            pl.BlockSpec((None, ATTN_HEADS, TM, HEAD_DIM), lambda b, i: (b, 0, i, 0)),
            pl.BlockSpec((None, None, KV_HEADS * HEAD_DIM, TM), lambda b, i: (b, i, 0, 0)),
            pl.BlockSpec((None, TM, KV_HEADS * HEAD_DIM), lambda b, i: (b, i, 0)),
            pl.BlockSpec((None, IDX_HEADS, TM, IDX_DIM), lambda b, i: (b, 0, i, 0)),
            pl.BlockSpec((None, None, IDX_DIM, TM), lambda b, i: (b, i, 0, 0)),
            pl.BlockSpec((None, TM, LANES), lambda b, i: (b, i, 0)),
            pl.BlockSpec((None, TM, POOL_WIDTH), lambda b, i: (b, i, 0)),
        ],
        out_shape=[
            jax.ShapeDtypeStruct((B, ATTN_HEADS, S, HEAD_DIM), BF16),
            jax.ShapeDtypeStruct((B, nt, KV_HEADS * HEAD_DIM, TM), BF16),
            jax.ShapeDtypeStruct((B, S, KV_HEADS * HEAD_DIM), BF16),
            jax.ShapeDtypeStruct((B, IDX_HEADS, S, IDX_DIM), BF16),
            jax.ShapeDtypeStruct((B, nt, IDX_DIM, TM), BF16),
            jax.ShapeDtypeStruct((B, S, LANES), F32),
            jax.ShapeDtypeStruct((B, S, POOL_WIDTH), BF16),
        ],
        scratch_shapes=[pltpu.VMEM((POOL_HALO + TM, POOL_WIDTH), F32)],
        compiler_params=pltpu.CompilerParams(
            dimension_semantics=("parallel", "arbitrary"),
            vmem_limit_bytes=VMEM_LIMIT_BYTES),
        name="proj_rope_pool",
    )(x, pos, g, w, freq, sa, sb, wp, ps)


def _attn_kernel(q_ref, kt_ref, v_ref, qi_ref, kit_ref, wi_ref, o_ref, key_scr, *, n_keep):
    j = pl.program_id(1)
    t0 = j * TQ
    nch = (t0 + TQ + KC - 1) // KC
    row_t = t0 + lax.broadcasted_iota(I32, (TQ, 1), 0)
    lane_idx = lax.broadcasted_iota(I32, (TQ, KC), 1)
    lane_tile = lax.broadcasted_iota(I32, (TQ, LANES), 1)
    wv = wi_ref[...]
    w_cols = [wv[:, W_LANE + h:W_LANE + h + 1] for h in range(IDX_HEADS)]

    def score_chunk(c, carry):
        kic = kit_ref[c]
        acc = jnp.zeros((TQ, KC), F32)
        for h in range(IDX_HEADS):
            logit = jnp.dot(qi_ref[h], kic, preferred_element_type=F32)
            acc = acc + w_cols[h] * jnp.maximum(logit, 0.0)
        bits = pltpu.bitcast(acc, I32)
        bits = jnp.where(bits == INT_MIN, 0, bits)
        key = bits ^ ((bits >> 31) & 0x7FFFFFFF)
        key_scr[c] = jnp.where(lane_idx + c * KC <= row_t, key, INT_MIN)
        return carry

    lax.fori_loop(0, nch, score_chunk, 0)

    def count(pred):
        def body(c, part):
            k = key_scr[c]
            for m in range(KC // LANES):
                idx = lane_tile + (c * KC + m * LANES)
                part = part + jnp.where(pred(k[:, m * LANES:(m + 1) * LANES], idx), 1.0, 0.0)
            return part
        part = lax.fori_loop(0, nch, body, jnp.zeros((TQ, LANES), F32))
        return jnp.broadcast_to(jnp.sum(part, axis=1, keepdims=True), (TQ, LANES))

    def chunk_wide(rep):
        return jnp.concatenate([rep] * (KC // LANES), axis=1)

    def select_threshold():
        def step(s, thr):
            cand = thr ^ lax.shift_left(jnp.int32(1), 31 - s)
            cnt = count(lambda k, _: k >= cand)
            return jnp.where(cnt >= n_keep, cand, thr)
        return lax.fori_loop(0, 32, step, jnp.full((TQ, LANES), INT_MIN, I32))

    needs_select = t0 + TQ > n_keep
    thr = lax.cond(needs_select, select_threshold, lambda: jnp.full((TQ, LANES), INT_MIN, I32))
    thr = jnp.maximum(thr, INT_MIN + 1)
    thr_w = chunk_wide(thr)

    n_ge = count(lambda k, _: k >= thr)
    has_tie = jnp.max(n_ge) > n_keep

    @pl.when(has_tie)
    def _():
        need = n_keep - count(lambda k, _: k > thr)

        def step(s, cut):
            cand = cut | lax.shift_left(jnp.int32(1), 11 - s)
            cnt = count(lambda k, idx: jnp.where(k == thr, idx, KC * 8) < cand)
            return jnp.where(cnt < need, cand, cut)
        cut = lax.fori_loop(0, 12, step, jnp.zeros((TQ, LANES), I32))
        cut_w = chunk_wide(cut)

        def fix(c, carry):
            k = key_scr[c]
            tie_idx = jnp.where(k == thr_w, lane_idx + c * KC, -1)
            key_scr[c] = jnp.where(tie_idx > cut_w, INT_MIN, k)
            return carry
        lax.fori_loop(0, nch, fix, 0)

    rows = HEADS_PER_KV * TQ
    outs = []
    for g in range(KV_HEADS):
        qg = q_ref[g * HEADS_PER_KV:(g + 1) * HEADS_PER_KV].reshape(rows, HEAD_DIM)

        def att_chunk(c, carry, g=g, qg=qg):
            m_run, l_run, acc = carry
            ktc = kt_ref[c, g * HEAD_DIM:(g + 1) * HEAD_DIM, :]
            vc = v_ref[pl.ds(pl.multiple_of(c * KC, KC), KC), g * HEAD_DIM:(g + 1) * HEAD_DIM]
            bias = jnp.where(key_scr[c] >= thr_w, 0.0, NEG_BIG)
            s = jnp.dot(qg, ktc, preferred_element_type=F32)
            s = s + jnp.concatenate([bias] * HEADS_PER_KV, axis=0)
            m_new = jnp.maximum(m_run, jnp.max(s, axis=1, keepdims=True))
            alpha = jnp.exp(m_run - m_new)
            p = jnp.exp(s - m_new)
            l_new = alpha * l_run + jnp.sum(p, axis=1, keepdims=True)
            acc = alpha * acc + jnp.dot(p.astype(BF16), vc, preferred_element_type=F32)
            return m_new, l_new, acc

        init = (jnp.full((rows, 1), NEG_BIG, F32), jnp.zeros((rows, 1), F32),
                jnp.zeros((rows, HEAD_DIM), F32))
        _, l_fin, acc = lax.fori_loop(0, nch, att_chunk, init)
        og = acc / l_fin
        outs += [og[r * TQ:(r + 1) * TQ] for r in range(HEADS_PER_KV)]
    o_ref[...] = jnp.concatenate(outs, axis=1).astype(BF16)


def _attn_call(q, kt, v, qi, kit, wi):
    B, _, S, _ = q.shape
    nc = S // KC
    n_keep = min(TOPK_MAX, S // 4)
    return pl.pallas_call(
        functools.partial(_attn_kernel, n_keep=n_keep),
        grid=(B, S // TQ),
        in_specs=[
            pl.BlockSpec((None, ATTN_HEADS, TQ, HEAD_DIM), lambda b, j: (b, 0, j, 0)),
            pl.BlockSpec((None, nc, KV_HEADS * HEAD_DIM, KC), lambda b, j: (b, 0, 0, 0)),
            pl.BlockSpec((None, S, KV_HEADS * HEAD_DIM), lambda b, j: (b, 0, 0)),
            pl.BlockSpec((None, IDX_HEADS, TQ, IDX_DIM), lambda b, j: (b, 0, j, 0)),
            pl.BlockSpec((None, nc, IDX_DIM, KC), lambda b, j: (b, 0, 0, 0)),
            pl.BlockSpec((None, TQ, LANES), lambda b, j: (b, j, 0)),
        ],
        out_specs=pl.BlockSpec((None, TQ, ATTN_WIDTH), lambda b, j: (b, j, 0)),
        out_shape=jax.ShapeDtypeStruct((B, S, ATTN_WIDTH), BF16),
        scratch_shapes=[pltpu.VMEM((nc, TQ, KC), I32)],
        compiler_params=pltpu.CompilerParams(
            dimension_semantics=("parallel", "arbitrary"),
            vmem_limit_bytes=VMEM_LIMIT_BYTES),
        name="dsa_attention",
    )(q, kt, v, qi, kit, wi)


def _tail_kernel(a_ref, pool_ref, x_ref, p_ref, wo_ref, wup_ref, wdn_ref, wg_ref, wpp_ref,
                 g1_ref, g2_ref, g3_ref, g4_ref, o_ref):
    mix = jnp.dot(a_ref[...], wo_ref[0:ATTN_WIDTH], preferred_element_type=F32)
    mix = mix + jnp.dot(pool_ref[...], wo_ref[ATTN_WIDTH:], preferred_element_type=F32)
    h = x_ref[...] + _rms(mix, g1_ref[...])

    hn = _rms(h, g2_ref[...]).astype(BF16)
    m = jnp.zeros((TT, D_MODEL), F32)
    for c in range(D_FF // FF_CHUNK):
        u = jnp.dot(hn, wup_ref[:, c * FF_CHUNK:(c + 1) * FF_CHUNK], preferred_element_type=F32)
        act = jnp.square(jnp.maximum(u, 0.0)).astype(BF16)
        m = m + jnp.dot(act, wdn_ref[c * FF_CHUNK:(c + 1) * FF_CHUNK], preferred_element_type=F32)
    h = h + _rms(m, g3_ref[...])

    gate = jax.nn.sigmoid(jnp.dot(h.astype(BF16), wg_ref[...], preferred_element_type=F32))
    e = jnp.dot(p_ref[...].astype(BF16), wpp_ref[...], preferred_element_type=F32)
    o_ref[...] = h + _rms(gate * e, g4_ref[...])


def _tail_call(attn, pool, x, p, wo, wup, wdn, wg, wpp, g1, g2, g3, g4):
    n = x.shape[0]
    tok = lambda width: pl.BlockSpec((TT, width), lambda i: (i, 0))
    const = lambda shape: pl.BlockSpec(shape, lambda i: (0, 0), pipeline_mode=pl.Buffered(1))
    return pl.pallas_call(
        _tail_kernel,
        grid=(n // TT,),
        in_specs=[
            tok(ATTN_WIDTH), tok(POOL_WIDTH), tok(D_MODEL), tok(PLE_DIM),
            const((D_MODEL, D_MODEL)), const((D_MODEL, D_FF)), const((D_FF, D_MODEL)),
            const((D_MODEL, D_MODEL)), const((PLE_DIM, D_MODEL)),
            const((1, D_MODEL)), const((1, D_MODEL)), const((1, D_MODEL)), const((1, D_MODEL)),
        ],
        out_specs=tok(D_MODEL),
        out_shape=jax.ShapeDtypeStruct((n, D_MODEL), F32),
        compiler_params=pltpu.CompilerParams(
            dimension_semantics=("parallel",),
            vmem_limit_bytes=VMEM_LIMIT_BYTES),
        name="oproj_mlp_ple",
    )(attn, pool, x, p, wo, wup, wdn, wg, wpp, g1, g2, g3, g4)


def _rope_lane_tables():
    inv_freq = ROPE_THETA ** (-jnp.arange(0, ROT_DIM, 2, dtype=F32) / ROT_DIM)
    lane = jnp.arange(LANES)
    in_head = lane % HEAD_DIM
    freq = jnp.where(in_head < ROT_DIM, inv_freq[lane % ROT_HALF], 0.0).astype(F32)
    sa = jnp.where(in_head < ROT_HALF, -1.0, 0.0).astype(F32)
    sb = jnp.where((in_head >= ROT_HALF) & (in_head < ROT_DIM), 1.0, 0.0).astype(F32)
    return freq[None], sa[None], sb[None]


def kernel(x, p, positions, g_mix_pre, w_in, w_pool, pool_scale, w_o, g_mix_post, g_mlp_pre,
           w_up, w_down, g_mlp_post, w_ple_gate, w_ple_proj, g_ple_post):
    B, S, D = x.shape
    depth = w_in.shape[0]
    freq, sa, sb = _rope_lane_tables()
    pos = positions.reshape(B, S, 1)
    h = x
    for i in range(depth):
        w_cat = jnp.concatenate(
            [w_in[i][:, :SRC_POOL],
             jnp.zeros((D, COL_POOL - SRC_POOL), w_in.dtype),
             w_in[i][:, SRC_POOL:]], axis=1).astype(BF16)
        q, kt, v, qi, kit, wi, pool = _proj_call(
            h, pos, g_mix_pre[i][None], w_cat, freq, sa, sb,
            w_pool[i].astype(BF16), pool_scale[i][None])
        attn = _attn_call(q, kt, v, qi, kit, wi)
        h = _tail_call(
            attn.reshape(B * S, ATTN_WIDTH), pool.reshape(B * S, POOL_WIDTH),
            h.reshape(B * S, D), p[i].reshape(B * S, PLE_DIM),
            w_o[i].astype(BF16), w_up[i].astype(BF16), w_down[i].astype(BF16),
            w_ple_gate[i].astype(BF16), w_ple_proj[i].astype(BF16),
            g_mix_post[i][None], g_mlp_pre[i][None], g_mlp_post[i][None], g_ple_post[i][None],
        ).reshape(B, S, D)
    return h
```

```python
import functools
import math

import jax
import jax.numpy as jnp
from jax import lax
from jax.experimental import pallas as pl
from jax.experimental.pallas import tpu as pltpu

F32 = jnp.float32
BF16 = jnp.bfloat16
I32 = jnp.int32

D_MODEL = 1024
PLE_DIM = 256
HEAD_DIM = 64
ATTN_HEADS = 8
KV_HEADS = 2
HEADS_PER_KV = ATTN_HEADS // KV_HEADS
ATTN_WIDTH = ATTN_HEADS * HEAD_DIM
KV_WIDTH = KV_HEADS * HEAD_DIM
IDX_HEADS = 4
IDX_DIM = 64
TOPK_MAX = 256
POOL_WINDOWS = (2, 4, 8, 16)
POOL_GROUPS = len(POOL_WINDOWS)
POOL_WIDTH = D_MODEL - ATTN_WIDTH
POOL_GROUP_DIM = POOL_WIDTH // POOL_GROUPS
POOL_HALO = max(POOL_WINDOWS)
D_FF = 4 * D_MODEL
ROPE_THETA = 500000.0
ROT_DIM = HEAD_DIM // 4
ROT_HALF = ROT_DIM // 2
EPS = 1e-6
ATTN_SCALE = HEAD_DIM ** -0.5
IDX_SCALE = IDX_DIM ** -0.5
IDX_W_SCALE = IDX_HEADS ** -0.5
LOG2E = math.log2(math.e)

LANES = 128
SUBLANES = 8
BF16_ROWS = 16
VMEM_LIMIT_BYTES = 56 * 1024 * 1024

COL_Q = 0
COL_K = COL_Q + ATTN_WIDTH
COL_V = COL_K + KV_WIDTH
COL_QI = COL_V + KV_WIDTH
COL_KI = COL_QI + IDX_HEADS * IDX_DIM
COL_W = COL_KI + IDX_DIM
W_LANE = COL_W - COL_KI
COL_POOL = COL_KI + LANES
PROJ_WIDTH = COL_POOL + POOL_WIDTH
SRC_POOL = COL_W + IDX_HEADS

TM = 512
KC = 512
TQ = LANES
TT = 256
FF_CHUNK = 1024
VT_ROWS = HEAD_DIM + BF16_ROWS
ACC_ROWS = 64

INT_MIN = -2 ** 31
NEG_BIG = -1e30


def _rms(x, g):
    ms = jnp.mean(x * x, axis=-1, keepdims=True)
    return x * lax.rsqrt(ms + EPS) * g


def _proj_kernel(x_ref, pos_ref, g_ref, w_ref, freq_ref, sa_ref, sb_ref, wp_ref, ps_ref,
                 qt_ref, k_ref, vt_ref, qit_ref, ki_ref, wt_ref, pool_ref, zp_scr):
    i = pl.program_id(1)
    hn = _rms(x_ref[...], g_ref[...]).astype(BF16)
    z = jnp.dot(hn, w_ref[...], preferred_element_type=F32)

    ang = pos_ref[...].astype(F32) * freq_ref[...]
    cos = jnp.cos(ang)
    sin = jnp.sin(ang)
    sa = sin * sa_ref[...]
    sb = sin * sb_ref[...]

    def rope(t):
        up = pltpu.roll(t, LANES - ROT_HALF, 1)
        dn = pltpu.roll(t, ROT_HALF, 1)
        return t * cos + up * sa + dn * sb

    def store_heads_transposed(dst_ref, t, first_head):
        tt = t.T.astype(BF16)
        for jj in range(TM // TQ):
            for hh in range(2):
                h = first_head + hh
                dst_ref[jj, :, h * TQ:(h + 1) * TQ] = (
                    tt[hh * HEAD_DIM:(hh + 1) * HEAD_DIM, jj * TQ:(jj + 1) * TQ])

    for m in range(ATTN_WIDTH // LANES):
        t = rope(z[:, COL_Q + m * LANES:COL_Q + (m + 1) * LANES]) * (ATTN_SCALE * LOG2E)
        store_heads_transposed(qt_ref, t, 2 * m)
    k_ref[...] = rope(z[:, COL_K:COL_V]).astype(BF16)
    v_t = z[:, COL_V:COL_QI].T.astype(BF16)
    for g in range(KV_HEADS):
        vt_ref[g, 0:HEAD_DIM] = v_t[g * HEAD_DIM:(g + 1) * HEAD_DIM]
        vt_ref[g, HEAD_DIM:VT_ROWS] = jnp.ones((BF16_ROWS, TM), BF16)
    for m in range(IDX_HEADS * IDX_DIM // LANES):
        t = rope(z[:, COL_QI + m * LANES:COL_QI + (m + 1) * LANES])
        store_heads_transposed(qit_ref, t, 2 * m)
    zki = z[:, COL_KI:COL_POOL]
    ki_ref[...] = rope(zki)[:, :IDX_DIM].astype(BF16)
    zki_t = zki.T
    for jj in range(TM // TQ):
        wt_ref[jj] = zki_t[W_LANE:W_LANE + SUBLANES, jj * TQ:(jj + 1) * TQ] * (IDX_W_SCALE * IDX_SCALE)

    zp = z[:, COL_POOL:]

    @pl.when(i == 0)
    def _():
        zp_scr[0:POOL_HALO] = jnp.zeros((POOL_HALO, POOL_WIDTH), F32)

    @pl.when(i > 0)
    def _():
        zp_scr[0:POOL_HALO] = zp_scr[TM:TM + POOL_HALO]

    zp_scr[POOL_HALO:POOL_HALO + TM] = zp
    t_idx = i * TM + lax.broadcasted_iota(I32, (TM, 1), 0)
    for g, win in enumerate(POOL_WINDOWS):
        c0 = g * POOL_GROUP_DIM
        acc = zp[:, c0:c0 + POOL_GROUP_DIM]
        for d in range(1, win):
            acc = acc + zp_scr[POOL_HALO - d:POOL_HALO - d + TM, c0:c0 + POOL_GROUP_DIM]
        cnt = jnp.minimum(t_idx + 1, win).astype(F32)
        r = acc / cnt - zp[:, c0:c0 + POOL_GROUP_DIM]
        y = jnp.dot(r.astype(BF16), wp_ref[g], preferred_element_type=F32)
        pool_ref[:, c0:c0 + POOL_GROUP_DIM] = (y * ps_ref[:, c0:c0 + POOL_GROUP_DIM]).astype(BF16)


def _proj_call(x, pos, g, w, freq, sa, sb, wp, ps):
    B, S, _ = x.shape
    nt = S // TM
    nq = S // TQ
    qpt = TM // TQ
    const2 = lambda b, i: (0, 0)
    return pl.pallas_call(
        _proj_kernel,
        grid=(B, nt),
        in_specs=[
            pl.BlockSpec((None, TM, D_MODEL), lambda b, i: (b, i, 0)),
            pl.BlockSpec((None, TM, 1), lambda b, i: (b, i, 0)),
            pl.BlockSpec((1, D_MODEL), const2),
            pl.BlockSpec((D_MODEL, PROJ_WIDTH), const2),
            pl.BlockSpec((1, LANES), const2),
            pl.BlockSpec((1, LANES), const2),
            pl.BlockSpec((1, LANES), const2),
            pl.BlockSpec((POOL_GROUPS, POOL_GROUP_DIM, POOL_GROUP_DIM), lambda b, i: (0, 0, 0)),
            pl.BlockSpec((1, POOL_WIDTH), const2),
        ],
        out_specs=[
            pl.BlockSpec((None, qpt, HEAD_DIM, ATTN_HEADS * TQ), lambda b, i: (b, i, 0, 0)),
            pl.BlockSpec((None, TM, KV_WIDTH), lambda b, i: (b, i, 0)),
            pl.BlockSpec((None, None, KV_HEADS, VT_ROWS, TM), lambda b, i: (b, i, 0, 0, 0)),
            pl.BlockSpec((None, qpt, IDX_DIM, IDX_HEADS * TQ), lambda b, i: (b, i, 0, 0)),
            pl.BlockSpec((None, TM, IDX_DIM), lambda b, i: (b, i, 0)),
            pl.BlockSpec((None, qpt, SUBLANES, TQ), lambda b, i: (b, i, 0, 0)),
            pl.BlockSpec((None, TM, POOL_WIDTH), lambda b, i: (b, i, 0)),
        ],
        out_shape=[
            jax.ShapeDtypeStruct((B, nq, HEAD_DIM, ATTN_HEADS * TQ), BF16),
            jax.ShapeDtypeStruct((B, S, KV_WIDTH), BF16),
            jax.ShapeDtypeStruct((B, nt, KV_HEADS, VT_ROWS, TM), BF16),
            jax.ShapeDtypeStruct((B, nq, IDX_DIM, IDX_HEADS * TQ), BF16),
            jax.ShapeDtypeStruct((B, S, IDX_DIM), BF16),
            jax.ShapeDtypeStruct((B, nq, SUBLANES, TQ), F32),
            jax.ShapeDtypeStruct((B, S, POOL_WIDTH), BF16),
        ],
        scratch_shapes=[pltpu.VMEM((POOL_HALO + TM, POOL_WIDTH), F32)],
        compiler_params=pltpu.CompilerParams(
            dimension_semantics=("parallel", "arbitrary"),
            vmem_limit_bytes=VMEM_LIMIT_BYTES),
        name="proj_rope_pool",
    )(x, pos, g, w, freq, sa, sb, wp, ps)


def _attn_kernel(qt_ref, k_ref, vt_ref, qit_ref, ki_ref, wt_ref, o_ref, key_scr, *, n_keep, idx_bits):
    j = pl.program_id(1)
    t0 = j * TQ
    nch = (t0 + TQ + KC - 1) // KC
    q_pos = t0 + lax.broadcasted_iota(I32, (KC, TQ), 1)
    k_off = lax.broadcasted_iota(I32, (KC, TQ), 0)
    wt = wt_ref[...]
    qit = qit_ref[...]

    def score_chunk(c, carry):
        kic = ki_ref[pl.ds(pl.multiple_of(c * KC, KC), KC), :]
        logit = jnp.dot(kic, qit, preferred_element_type=F32)
        acc = jnp.zeros((KC, TQ), F32)
        for h in range(IDX_HEADS):
            acc = acc + wt[h:h + 1, :] * jnp.maximum(logit[:, h * TQ:(h + 1) * TQ], 0.0)
        bits = pltpu.bitcast(acc, I32)
        bits = jnp.where(bits == INT_MIN, 0, bits)
        key = bits ^ ((bits >> 31) & 0x7FFFFFFF)
        key_scr[c] = jnp.where(k_off + c * KC <= q_pos, key, INT_MIN)
        return carry

    lax.fori_loop(0, nch, score_chunk, 0)

    def count(pred):
        def body(c, part):
            hit = jnp.where(pred(key_scr[c], k_off + c * KC), 1.0, 0.0)
            return part + jnp.sum(hit.reshape(KC // ACC_ROWS, ACC_ROWS, TQ), axis=0)
        part = lax.fori_loop(0, nch, body, jnp.zeros((ACC_ROWS, TQ), F32))
        return jnp.sum(part, axis=0, keepdims=True)

    def select_threshold():
        def step(s, thr):
            cand = thr ^ lax.shift_left(jnp.int32(1), 31 - s)
            cnt = count(lambda k, _: k >= cand)
            return jnp.where(cnt >= n_keep, cand, thr)
        return lax.fori_loop(0, 32, step, jnp.full((1, TQ), INT_MIN, I32))

    needs_select = t0 + TQ > n_keep
    thr = lax.cond(needs_select, select_threshold, lambda: jnp.full((1, TQ), INT_MIN, I32))
    thr = jnp.maximum(thr, INT_MIN + 1)

    n_ge = count(lambda k, _: k >= thr)
    has_tie = jnp.max(n_ge) > n_keep

    @pl.when(has_tie)
    def _():
        need = n_keep - count(lambda k, _: k > thr)

        def step(s, cut):
            cand = cut | lax.shift_left(jnp.int32(1), idx_bits - 1 - s)
            cnt = count(lambda k, idx: jnp.where(k == thr, idx, 1 << idx_bits) < cand)
            return jnp.where(cnt < need, cand, cut)
        cut = lax.fori_loop(0, idx_bits, step, jnp.zeros((1, TQ), I32))

        def fix(c, carry):
            k = key_scr[c]
            tie_idx = jnp.where(k == thr, k_off + c * KC, -1)
            key_scr[c] = jnp.where(tie_idx > cut, INT_MIN, k)
            return carry
        lax.fori_loop(0, nch, fix, 0)

    width = HEADS_PER_KV * TQ
    qt = qt_ref[...]
    eye = (lax.broadcasted_iota(I32, (TQ, width), 0)
           == lax.broadcasted_iota(I32, (TQ, width), 1) % TQ).astype(BF16)
    zeros = jnp.zeros((HEAD_DIM, width), BF16)
    rhs = []
    for g in range(KV_HEADS):
        rows = [zeros] * KV_HEADS
        rows[g] = qt[:, g * width:(g + 1) * width]
        rhs.append(jnp.concatenate(rows + [eye], axis=0))

    def scores(c):
        start = pl.multiple_of(c * KC, KC)
        bias = jnp.where(key_scr[c] >= thr, 0.0, NEG_BIG).astype(BF16)
        lhs = jnp.concatenate([k_ref[pl.ds(start, KC), :], bias], axis=1)
        return tuple(jnp.dot(lhs, rhs[g], preferred_element_type=F32) for g in range(KV_HEADS))

    def softmax_pv(c, s_pair, carry):
        out = []
        for g in range(KV_HEADS):
            m_run, l_run, acc = carry[g]
            s = s_pair[g]
            m_part = jnp.max(s.reshape(KC // ACC_ROWS, ACC_ROWS, width), axis=0)
            m_new = jnp.maximum(m_run, jnp.max(m_part, axis=0, keepdims=True))
            alpha = jnp.exp2(m_run - m_new)
            p = jnp.exp2(s - m_new).astype(BF16)
            pv = jnp.dot(vt_ref[c, g], p, preferred_element_type=F32)
            acc = alpha * acc + pv[0:HEAD_DIM]
            l_new = alpha * l_run + pv[HEAD_DIM:HEAD_DIM + 1]
            out.append((m_new, l_new, acc))
        return tuple(out)

    def att_chunk(c, state):
        s_cur, carry = state
        s_next = scores(c + 1)
        return s_next, softmax_pv(c, s_cur, carry)

    init = tuple((jnp.full((1, width), NEG_BIG, F32), jnp.zeros((1, width), F32),
                  jnp.zeros((HEAD_DIM, width), F32)) for _ in range(KV_HEADS))
    s_last, carry = lax.fori_loop(0, nch - 1, att_chunk, (scores(0), init))
    fin = softmax_pv(nch - 1, s_last, carry)
    heads = []
    for g in range(KV_HEADS):
        _, l_fin, acc = fin[g]
        og = acc / l_fin
        heads += [og[:, r * TQ:(r + 1) * TQ] for r in range(HEADS_PER_KV)]
    o_ref[...] = jnp.concatenate(heads, axis=0).T.astype(BF16)


def _attn_call(qt, k, vt, qit, ki, wt):
    B, S, _ = k.shape
    nc = S // KC
    n_keep = min(TOPK_MAX, S // 4)
    return pl.pallas_call(
        functools.partial(_attn_kernel, n_keep=n_keep, idx_bits=(S - 1).bit_length()),
        grid=(B, S // TQ),
        in_specs=[
            pl.BlockSpec((None, None, HEAD_DIM, ATTN_HEADS * TQ), lambda b, j: (b, j, 0, 0)),
            pl.BlockSpec((None, S, KV_WIDTH), lambda b, j: (b, 0, 0)),
            pl.BlockSpec((None, nc, KV_HEADS, VT_ROWS, KC), lambda b, j: (b, 0, 0, 0, 0)),
            pl.BlockSpec((None, None, IDX_DIM, IDX_HEADS * TQ), lambda b, j: (b, j, 0, 0)),
            pl.BlockSpec((None, S, IDX_DIM), lambda b, j: (b, 0, 0)),
            pl.BlockSpec((None, None, SUBLANES, TQ), lambda b, j: (b, j, 0, 0)),
        ],
        out_specs=pl.BlockSpec((None, TQ, ATTN_WIDTH), lambda b, j: (b, j, 0)),
        out_shape=jax.ShapeDtypeStruct((B, S, ATTN_WIDTH), BF16),
        scratch_shapes=[pltpu.VMEM((nc, KC, TQ), I32)],
        compiler_params=pltpu.CompilerParams(
            dimension_semantics=("parallel", "arbitrary"),
            vmem_limit_bytes=VMEM_LIMIT_BYTES),
        name="dsa_attention",
    )(qt, k, vt, qit, ki, wt)


def _tail_kernel(a_ref, pool_ref, x_ref, p_ref, wo_ref, wup_ref, wdn_ref, wg_ref, wpp_ref,
                 g1_ref, g2_ref, g3_ref, g4_ref, o_ref):
    mix = jnp.dot(a_ref[...], wo_ref[0:ATTN_WIDTH], preferred_element_type=F32)
    mix = mix + jnp.dot(pool_ref[...], wo_ref[ATTN_WIDTH:], preferred_element_type=F32)
    h = x_ref[...] + _rms(mix, g1_ref[...])

    hn = _rms(h, g2_ref[...]).astype(BF16)
    m = jnp.zeros((TT, D_MODEL), F32)
    for c in range(D_FF // FF_CHUNK):
        u = jnp.dot(hn, wup_ref[:, c * FF_CHUNK:(c + 1) * FF_CHUNK], preferred_element_type=F32)
        act = jnp.square(jnp.maximum(u, 0.0)).astype(BF16)
        m = m + jnp.dot(act, wdn_ref[c * FF_CHUNK:(c + 1) * FF_CHUNK], preferred_element_type=F32)
    h = h + _rms(m, g3_ref[...])

    gate = jax.nn.sigmoid(jnp.dot(h.astype(BF16), wg_ref[...], preferred_element_type=F32))
    e = jnp.dot(p_ref[...].astype(BF16), wpp_ref[...], preferred_element_type=F32)
    o_ref[...] = h + _rms(gate * e, g4_ref[...])


def _tail_call(attn, pool, x, p, wo, wup, wdn, wg, wpp, g1, g2, g3, g4):
    n = x.shape[0]
    tok = lambda width: pl.BlockSpec((TT, width), lambda i: (i, 0))
    const = lambda shape: pl.BlockSpec(shape, lambda i: (0, 0), pipeline_mode=pl.Buffered(1))
    return pl.pallas_call(
        _tail_kernel,
        grid=(n // TT,),
        in_specs=[
            tok(ATTN_WIDTH), tok(POOL_WIDTH), tok(D_MODEL), tok(PLE_DIM),
            const((D_MODEL, D_MODEL)), const((D_MODEL, D_FF)), const((D_FF, D_MODEL)),
            const((D_MODEL, D_MODEL)), const((PLE_DIM, D_MODEL)),
            const((1, D_MODEL)), const((1, D_MODEL)), const((1, D_MODEL)), const((1, D_MODEL)),
        ],
        out_specs=tok(D_MODEL),
        out_shape=jax.ShapeDtypeStruct((n, D_MODEL), F32),
        compiler_params=pltpu.CompilerParams(
            dimension_semantics=("parallel",),
            vmem_limit_bytes=VMEM_LIMIT_BYTES),
        name="oproj_mlp_ple",
    )(attn, pool, x, p, wo, wup, wdn, wg, wpp, g1, g2, g3, g4)


def _rope_lane_tables():
    inv_freq = ROPE_THETA ** (-jnp.arange(0, ROT_DIM, 2, dtype=F32) / ROT_DIM)
    lane = jnp.arange(LANES)
    in_head = lane % HEAD_DIM
    freq = jnp.where(in_head < ROT_DIM, inv_freq[lane % ROT_HALF], 0.0).astype(F32)
    sa = jnp.where(in_head < ROT_HALF, -1.0, 0.0).astype(F32)
    sb = jnp.where((in_head >= ROT_HALF) & (in_head < ROT_DIM), 1.0, 0.0).astype(F32)
    return freq[None], sa[None], sb[None]


def kernel(x, p, positions, g_mix_pre, w_in, w_pool, pool_scale, w_o, g_mix_post, g_mlp_pre,
           w_up, w_down, g_mlp_post, w_ple_gate, w_ple_proj, g_ple_post):
    B, S, D = x.shape
    depth = w_in.shape[0]
    freq, sa, sb = _rope_lane_tables()
    pos = positions.reshape(B, S, 1)
    h = x
    for i in range(depth):
        w_cat = jnp.concatenate(
            [w_in[i][:, :SRC_POOL],
             jnp.zeros((D, COL_POOL - SRC_POOL), w_in.dtype),
             w_in[i][:, SRC_POOL:]], axis=1).astype(BF16)
        qt, k, vt, qit, ki, wt, pool = _proj_call(
            h, pos, g_mix_pre[i][None], w_cat, freq, sa, sb,
            w_pool[i].astype(BF16), pool_scale[i][None])
        attn = _attn_call(qt, k, vt, qit, ki, wt)
        h = _tail_call(
            attn.reshape(B * S, ATTN_WIDTH), pool.reshape(B * S, POOL_WIDTH),
            h.reshape(B * S, D), p[i].reshape(B * S, PLE_DIM),
            w_o[i].astype(BF16), w_up[i].astype(BF16), w_down[i].astype(BF16),
            w_ple_gate[i].astype(BF16), w_ple_proj[i].astype(BF16),
            g_mix_post[i][None], g_mlp_pre[i][None], g_mlp_post[i][None], g_ple_post[i][None],
        ).reshape(B, S, D)
    return h
```

```python
import functools
import math

import jax
import jax.numpy as jnp
from jax import lax
from jax.experimental import pallas as pl
from jax.experimental.pallas import tpu as pltpu

F32 = jnp.float32
BF16 = jnp.bfloat16
I32 = jnp.int32

D_MODEL = 1024
PLE_DIM = 256
HEAD_DIM = 64
ATTN_HEADS = 8
KV_HEADS = 2
HEADS_PER_KV = ATTN_HEADS // KV_HEADS
ATTN_WIDTH = ATTN_HEADS * HEAD_DIM
KV_WIDTH = KV_HEADS * HEAD_DIM
IDX_HEADS = 4
IDX_DIM = 64
TOPK_MAX = 256
POOL_WINDOWS = (2, 4, 8, 16)
POOL_GROUPS = len(POOL_WINDOWS)
POOL_WIDTH = D_MODEL - ATTN_WIDTH
POOL_GROUP_DIM = POOL_WIDTH // POOL_GROUPS
POOL_HALO = max(POOL_WINDOWS)
D_FF = 4 * D_MODEL
ROPE_THETA = 500000.0
ROT_DIM = HEAD_DIM // 4
ROT_HALF = ROT_DIM // 2
EPS = 1e-6
ATTN_SCALE = HEAD_DIM ** -0.5
IDX_SCALE = IDX_DIM ** -0.5
IDX_W_SCALE = IDX_HEADS ** -0.5
LOG2E = math.log2(math.e)

LANES = 128
SUBLANES = 8
BF16_ROWS = 16
VMEM_LIMIT_BYTES = 56 * 1024 * 1024

COL_Q = 0
COL_K = COL_Q + ATTN_WIDTH
COL_V = COL_K + KV_WIDTH
COL_QI = COL_V + KV_WIDTH
COL_KI = COL_QI + IDX_HEADS * IDX_DIM
COL_W = COL_KI + IDX_DIM
W_LANE = COL_W - COL_KI
COL_POOL = COL_KI + LANES
PROJ_WIDTH = COL_POOL + POOL_WIDTH
SRC_POOL = COL_W + IDX_HEADS

TM = 512
KC = 512
TQ = LANES
TT = 256
FF_CHUNK = 1024
VT_ROWS = HEAD_DIM + BF16_ROWS
ACC_ROWS = 64

INT_MIN = -2 ** 31
NEG_BIG = -1e30


def _rms(x, g):
    ms = jnp.mean(x * x, axis=-1, keepdims=True)
    return x * lax.rsqrt(ms + EPS) * g


def _proj_kernel(x_ref, pos_ref, g_ref, w_ref, freq_ref, sa_ref, sb_ref, wp_ref, ps_ref,
                 qt_ref, k_ref, vt_ref, qit_ref, ki_ref, wt_ref, pool_ref, zp_scr):
    i = pl.program_id(1)
    hn = _rms(x_ref[...], g_ref[...]).astype(BF16)
    z = jnp.dot(hn, w_ref[...], preferred_element_type=F32)

    ang = pos_ref[...].astype(F32) * freq_ref[...]
    cos = jnp.cos(ang)
    sin = jnp.sin(ang)
    sa = sin * sa_ref[...]
    sb = sin * sb_ref[...]

    def rope(t):
        up = pltpu.roll(t, LANES - ROT_HALF, 1)
        dn = pltpu.roll(t, ROT_HALF, 1)
        return t * cos + up * sa + dn * sb

    def store_heads_transposed(dst_ref, t, first_head):
        tt = t.T.astype(BF16)
        for jj in range(TM // TQ):
            for hh in range(2):
                h = first_head + hh
                dst_ref[jj, :, h * TQ:(h + 1) * TQ] = (
                    tt[hh * HEAD_DIM:(hh + 1) * HEAD_DIM, jj * TQ:(jj + 1) * TQ])

    for m in range(ATTN_WIDTH // LANES):
        t = rope(z[:, COL_Q + m * LANES:COL_Q + (m + 1) * LANES]) * (ATTN_SCALE * LOG2E)
        store_heads_transposed(qt_ref, t, 2 * m)
    k_ref[...] = rope(z[:, COL_K:COL_V]).astype(BF16)
    v_t = z[:, COL_V:COL_QI].T.astype(BF16)
    for g in range(KV_HEADS):
        vt_ref[g, 0:HEAD_DIM] = v_t[g * HEAD_DIM:(g + 1) * HEAD_DIM]
        vt_ref[g, HEAD_DIM:VT_ROWS] = jnp.ones((BF16_ROWS, TM), BF16)
    for m in range(IDX_HEADS * IDX_DIM // LANES):
        t = rope(z[:, COL_QI + m * LANES:COL_QI + (m + 1) * LANES])
        store_heads_transposed(qit_ref, t, 2 * m)
    zki = z[:, COL_KI:COL_POOL]
    ki_ref[...] = rope(zki)[:, :IDX_DIM].astype(BF16)
    zki_t = zki.T
    for jj in range(TM // TQ):
        wt_ref[jj] = zki_t[W_LANE:W_LANE + SUBLANES, jj * TQ:(jj + 1) * TQ] * (IDX_W_SCALE * IDX_SCALE)

    zp = z[:, COL_POOL:]

    @pl.when(i == 0)
    def _():
        zp_scr[0:POOL_HALO] = jnp.zeros((POOL_HALO, POOL_WIDTH), F32)

    @pl.when(i > 0)
    def _():
        zp_scr[0:POOL_HALO] = zp_scr[TM:TM + POOL_HALO]

    zp_scr[POOL_HALO:POOL_HALO + TM] = zp
    t_idx = i * TM + lax.broadcasted_iota(I32, (TM, 1), 0)
    for g, win in enumerate(POOL_WINDOWS):
        c0 = g * POOL_GROUP_DIM
        acc = zp[:, c0:c0 + POOL_GROUP_DIM]
        for d in range(1, win):
            acc = acc + zp_scr[POOL_HALO - d:POOL_HALO - d + TM, c0:c0 + POOL_GROUP_DIM]
        cnt = jnp.minimum(t_idx + 1, win).astype(F32)
        r = acc / cnt - zp[:, c0:c0 + POOL_GROUP_DIM]
        y = jnp.dot(r.astype(BF16), wp_ref[g], preferred_element_type=F32)
        pool_ref[:, c0:c0 + POOL_GROUP_DIM] = (y * ps_ref[:, c0:c0 + POOL_GROUP_DIM]).astype(BF16)


def _proj_call(x, pos, g, w, freq, sa, sb, wp, ps):
    B, S, _ = x.shape
    nt = S // TM
    nq = S // TQ
    qpt = TM // TQ
    const2 = lambda b, i: (0, 0)
    return pl.pallas_call(
        _proj_kernel,
        grid=(B, nt),
        in_specs=[
            pl.BlockSpec((None, TM, D_MODEL), lambda b, i: (b, i, 0)),
            pl.BlockSpec((None, TM, 1), lambda b, i: (b, i, 0)),
            pl.BlockSpec((1, D_MODEL), const2),
            pl.BlockSpec((D_MODEL, PROJ_WIDTH), const2),
            pl.BlockSpec((1, LANES), const2),
            pl.BlockSpec((1, LANES), const2),
            pl.BlockSpec((1, LANES), const2),
            pl.BlockSpec((POOL_GROUPS, POOL_GROUP_DIM, POOL_GROUP_DIM), lambda b, i: (0, 0, 0)),
            pl.BlockSpec((1, POOL_WIDTH), const2),
        ],
        out_specs=[
            pl.BlockSpec((None, qpt, HEAD_DIM, ATTN_HEADS * TQ), lambda b, i: (b, i, 0, 0)),
            pl.BlockSpec((None, TM, KV_WIDTH), lambda b, i: (b, i, 0)),
            pl.BlockSpec((None, None, KV_HEADS, VT_ROWS, TM), lambda b, i: (b, i, 0, 0, 0)),
            pl.BlockSpec((None, qpt, IDX_DIM, IDX_HEADS * TQ), lambda b, i: (b, i, 0, 0)),
            pl.BlockSpec((None, TM, IDX_DIM), lambda b, i: (b, i, 0)),
            pl.BlockSpec((None, qpt, SUBLANES, TQ), lambda b, i: (b, i, 0, 0)),
            pl.BlockSpec((None, TM, POOL_WIDTH), lambda b, i: (b, i, 0)),
        ],
        out_shape=[
            jax.ShapeDtypeStruct((B, nq, HEAD_DIM, ATTN_HEADS * TQ), BF16),
            jax.ShapeDtypeStruct((B, S, KV_WIDTH), BF16),
            jax.ShapeDtypeStruct((B, nt, KV_HEADS, VT_ROWS, TM), BF16),
            jax.ShapeDtypeStruct((B, nq, IDX_DIM, IDX_HEADS * TQ), BF16),
            jax.ShapeDtypeStruct((B, S, IDX_DIM), BF16),
            jax.ShapeDtypeStruct((B, nq, SUBLANES, TQ), F32),
            jax.ShapeDtypeStruct((B, S, POOL_WIDTH), BF16),
        ],
        scratch_shapes=[pltpu.VMEM((POOL_HALO + TM, POOL_WIDTH), F32)],
        compiler_params=pltpu.CompilerParams(
            dimension_semantics=("parallel", "arbitrary"),
            vmem_limit_bytes=VMEM_LIMIT_BYTES),
        name="proj_rope_pool",
    )(x, pos, g, w, freq, sa, sb, wp, ps)


def _attn_kernel(qt_ref, k_ref, vt_ref, qit_ref, ki_ref, wt_ref, tril_ref, o_ref, key_scr, *, n_keep):
    j = pl.program_id(1)
    t0 = j * TQ
    nch = (t0 + TQ + KC - 1) // KC
    q_pos = t0 + lax.broadcasted_iota(I32, (KC, TQ), 1)
    k_off = lax.broadcasted_iota(I32, (KC, TQ), 0)
    wt = wt_ref[...]
    qit = qit_ref[...]

    def score_chunk(c, carry):
        kic = ki_ref[pl.ds(pl.multiple_of(c * KC, KC), KC), :]
        logit = jnp.dot(kic, qit, preferred_element_type=F32)
        acc = jnp.zeros((KC, TQ), F32)
        for h in range(IDX_HEADS):
            acc = acc + wt[h:h + 1, :] * jnp.maximum(logit[:, h * TQ:(h + 1) * TQ], 0.0)
        bits = pltpu.bitcast(acc, I32)
        bits = jnp.where(bits == INT_MIN, 0, bits)
        key = bits ^ ((bits >> 31) & 0x7FFFFFFF)
        key_scr[c] = jnp.where(k_off + c * KC <= q_pos, key, INT_MIN)
        return carry

    lax.fori_loop(0, nch, score_chunk, 0)

    def count(pred):
        def body(c, part):
            hit = jnp.where(pred(key_scr[c]), 1.0, 0.0)
            return part + jnp.sum(hit.reshape(KC // ACC_ROWS, ACC_ROWS, TQ), axis=0)
        part = lax.fori_loop(0, nch, body, jnp.zeros((ACC_ROWS, TQ), F32))
        return jnp.sum(part, axis=0, keepdims=True)

    def select_threshold():
        def step(s, thr):
            cand = thr ^ lax.shift_left(jnp.int32(1), 31 - s)
            cnt = count(lambda k: k >= cand)
            return jnp.where(cnt >= n_keep, cand, thr)
        return lax.fori_loop(0, 32, step, jnp.full((1, TQ), INT_MIN, I32))

    needs_select = t0 + TQ > n_keep
    thr = lax.cond(needs_select, select_threshold, lambda: jnp.full((1, TQ), INT_MIN, I32))
    thr = jnp.maximum(thr, INT_MIN + 1)

    @pl.when(needs_select)
    def _():
        need = n_keep - count(lambda k: k > thr)
        tril = tril_ref[...]

        def drop_late_ties(c, seen):
            k = key_scr[c]
            tie = k == thr
            rank = seen + jnp.dot(tril, jnp.where(tie, 1.0, 0.0).astype(BF16),
                                  preferred_element_type=F32)
            key_scr[c] = jnp.where(tie, jnp.where(rank > need, INT_MIN, k), k)
            return rank[KC - 1:KC, :]
        lax.fori_loop(0, nch, drop_late_ties, jnp.zeros((1, TQ), F32))

    width = HEADS_PER_KV * TQ
    qt = qt_ref[...]
    eye = (lax.broadcasted_iota(I32, (TQ, width), 0)
           == lax.broadcasted_iota(I32, (TQ, width), 1) % TQ).astype(BF16)
    zeros = jnp.zeros((HEAD_DIM, width), BF16)
    rhs = []
    for g in range(KV_HEADS):
        rows = [zeros] * KV_HEADS
        rows[g] = qt[:, g * width:(g + 1) * width]
        rhs.append(jnp.concatenate(rows + [eye], axis=0))

    def scores(c):
        start = pl.multiple_of(c * KC, KC)
        bias = jnp.where(key_scr[c] >= thr, 0.0, NEG_BIG).astype(BF16)
        lhs = jnp.concatenate([k_ref[pl.ds(start, KC), :], bias], axis=1)
        return tuple(jnp.dot(lhs, rhs[g], preferred_element_type=F32) for g in range(KV_HEADS))

    def softmax_pv(c, s_pair, carry):
        out = []
        for g in range(KV_HEADS):
            m_run, l_run, acc = carry[g]
            s = s_pair[g]
            m_part = jnp.max(s.reshape(KC // ACC_ROWS, ACC_ROWS, width), axis=0)
            m_new = jnp.maximum(m_run, jnp.max(m_part, axis=0, keepdims=True))
            alpha = jnp.exp2(m_run - m_new)
            p = jnp.exp2(s - m_new).astype(BF16)
            pv = jnp.dot(vt_ref[c, g], p, preferred_element_type=F32)
            acc = alpha * acc + pv[0:HEAD_DIM]
            l_new = alpha * l_run + pv[HEAD_DIM:HEAD_DIM + 1]
            out.append((m_new, l_new, acc))
        return tuple(out)

    def att_chunk(c, state):
        s_cur, carry = state
        s_next = scores(c + 1)
        return s_next, softmax_pv(c, s_cur, carry)

    init = tuple((jnp.full((1, width), NEG_BIG, F32), jnp.zeros((1, width), F32),
                  jnp.zeros((HEAD_DIM, width), F32)) for _ in range(KV_HEADS))
    s_last, carry = lax.fori_loop(0, nch - 1, att_chunk, (scores(0), init))
    fin = softmax_pv(nch - 1, s_last, carry)
    heads = []
    for g in range(KV_HEADS):
        _, l_fin, acc = fin[g]
        og = acc / l_fin
        heads += [og[:, r * TQ:(r + 1) * TQ] for r in range(HEADS_PER_KV)]
    o_ref[...] = jnp.concatenate(heads, axis=0).T.astype(BF16)


def _attn_call(qt, k, vt, qit, ki, wt):
    B, S, _ = k.shape
    nc = S // KC
    n_keep = min(TOPK_MAX, S // 4)
    return pl.pallas_call(
        functools.partial(_attn_kernel, n_keep=n_keep),
        grid=(B, S // TQ),
        in_specs=[
            pl.BlockSpec((None, None, HEAD_DIM, ATTN_HEADS * TQ), lambda b, j: (b, j, 0, 0)),
            pl.BlockSpec((None, S, KV_WIDTH), lambda b, j: (b, 0, 0)),
            pl.BlockSpec((None, nc, KV_HEADS, VT_ROWS, KC), lambda b, j: (b, 0, 0, 0, 0)),
            pl.BlockSpec((None, None, IDX_DIM, IDX_HEADS * TQ), lambda b, j: (b, j, 0, 0)),
            pl.BlockSpec((None, S, IDX_DIM), lambda b, j: (b, 0, 0)),
            pl.BlockSpec((None, None, SUBLANES, TQ), lambda b, j: (b, j, 0, 0)),
            pl.BlockSpec((KC, KC), lambda b, j: (0, 0)),
        ],
        out_specs=pl.BlockSpec((None, TQ, ATTN_WIDTH), lambda b, j: (b, j, 0)),
        out_shape=jax.ShapeDtypeStruct((B, S, ATTN_WIDTH), BF16),
        scratch_shapes=[pltpu.VMEM((nc, KC, TQ), I32)],
        compiler_params=pltpu.CompilerParams(
            dimension_semantics=("parallel", "arbitrary"),
            vmem_limit_bytes=VMEM_LIMIT_BYTES),
        name="dsa_attention",
    )(qt, k, vt, qit, ki, wt, jnp.tril(jnp.ones((KC, KC), BF16)))


def _tail_kernel(a_ref, pool_ref, x_ref, p_ref, wo_ref, wup_ref, wdn_ref, wg_ref, wpp_ref,
                 g1_ref, g2_ref, g3_ref, g4_ref, o_ref):
    mix = jnp.dot(a_ref[...], wo_ref[0:ATTN_WIDTH], preferred_element_type=F32)
    mix = mix + jnp.dot(pool_ref[...], wo_ref[ATTN_WIDTH:], preferred_element_type=F32)
    h = x_ref[...] + _rms(mix, g1_ref[...])

    hn = _rms(h, g2_ref[...]).astype(BF16)
    m = jnp.zeros((TT, D_MODEL), F32)
    for c in range(D_FF // FF_CHUNK):
        u = jnp.dot(hn, wup_ref[:, c * FF_CHUNK:(c + 1) * FF_CHUNK], preferred_element_type=F32)
        act = jnp.square(jnp.maximum(u, 0.0)).astype(BF16)
        m = m + jnp.dot(act, wdn_ref[c * FF_CHUNK:(c + 1) * FF_CHUNK], preferred_element_type=F32)
    h = h + _rms(m, g3_ref[...])

    gate = jax.nn.sigmoid(jnp.dot(h.astype(BF16), wg_ref[...], preferred_element_type=F32))
    e = jnp.dot(p_ref[...].astype(BF16), wpp_ref[...], preferred_element_type=F32)
    o_ref[...] = h + _rms(gate * e, g4_ref[...])


def _tail_call(attn, pool, x, p, wo, wup, wdn, wg, wpp, g1, g2, g3, g4):
    n = x.shape[0]
    tok = lambda width: pl.BlockSpec((TT, width), lambda i: (i, 0))
    const = lambda shape: pl.BlockSpec(shape, lambda i: (0, 0), pipeline_mode=pl.Buffered(1))
    return pl.pallas_call(
        _tail_kernel,
        grid=(n // TT,),
        in_specs=[
            tok(ATTN_WIDTH), tok(POOL_WIDTH), tok(D_MODEL), tok(PLE_DIM),
            const((D_MODEL, D_MODEL)), const((D_MODEL, D_FF)), const((D_FF, D_MODEL)),
            const((D_MODEL, D_MODEL)), const((PLE_DIM, D_MODEL)),
            const((1, D_MODEL)), const((1, D_MODEL)), const((1, D_MODEL)), const((1, D_MODEL)),
        ],
        out_specs=tok(D_MODEL),
        out_shape=jax.ShapeDtypeStruct((n, D_MODEL), F32),
        compiler_params=pltpu.CompilerParams(
            dimension_semantics=("parallel",),
            vmem_limit_bytes=VMEM_LIMIT_BYTES),
        name="oproj_mlp_ple",
    )(attn, pool, x, p, wo, wup, wdn, wg, wpp, g1, g2, g3, g4)


def _rope_lane_tables():
    inv_freq = ROPE_THETA ** (-jnp.arange(0, ROT_DIM, 2, dtype=F32) / ROT_DIM)
    lane = jnp.arange(LANES)
    in_head = lane % HEAD_DIM
    freq = jnp.where(in_head < ROT_DIM, inv_freq[lane % ROT_HALF], 0.0).astype(F32)
    sa = jnp.where(in_head < ROT_HALF, -1.0, 0.0).astype(F32)
    sb = jnp.where((in_head >= ROT_HALF) & (in_head < ROT_DIM), 1.0, 0.0).astype(F32)
    return freq[None], sa[None], sb[None]


def kernel(x, p, positions, g_mix_pre, w_in, w_pool, pool_scale, w_o, g_mix_post, g_mlp_pre,
           w_up, w_down, g_mlp_post, w_ple_gate, w_ple_proj, g_ple_post):
    B, S, D = x.shape
    depth = w_in.shape[0]
    freq, sa, sb = _rope_lane_tables()
    pos = positions.reshape(B, S, 1)
    h = x
    for i in range(depth):
        w_cat = jnp.concatenate(
            [w_in[i][:, :SRC_POOL],
             jnp.zeros((D, COL_POOL - SRC_POOL), w_in.dtype),
             w_in[i][:, SRC_POOL:]], axis=1).astype(BF16)
        qt, k, vt, qit, ki, wt, pool = _proj_call(
            h, pos, g_mix_pre[i][None], w_cat, freq, sa, sb,
            w_pool[i].astype(BF16), pool_scale[i][None])
        attn = _attn_call(qt, k, vt, qit, ki, wt)
        h = _tail_call(
            attn.reshape(B * S, ATTN_WIDTH), pool.reshape(B * S, POOL_WIDTH),
            h.reshape(B * S, D), p[i].reshape(B * S, PLE_DIM),
            w_o[i].astype(BF16), w_up[i].astype(BF16), w_down[i].astype(BF16),
            w_ple_gate[i].astype(BF16), w_ple_proj[i].astype(BF16),
            g_mix_post[i][None], g_mlp_pre[i][None], g_mlp_post[i][None], g_ple_post[i][None],
        ).reshape(B, S, D)
    return h
```
